```python
import jax, jax.numpy as jnp
from jax import lax
import numpy as np

D_MODEL = 1024
BATCH = 4
SEQ = 8192
DEPTH = 4
DEC_BATCH = 16
DEC_SEQ = 16
PAST_LEN = 2048

CHUNK = 64
N_A_LAYERS = DEPTH // 2
N_B_LAYERS = DEPTH - N_A_LAYERS
EPS = 1e-6
A_HEADS = 8
Q_LORA = 384
KV_LORA = 256
QK_NOPE = 128
QK_ROPE = 64
V_HEAD = 128
QK_HEAD = QK_NOPE + QK_ROPE
ROPE_THETA = 10000.0
MLA_SCALE = QK_HEAD ** -0.5
ATTN_Q_BLOCK = 128
B_HEADS = 16
B_KV_HEADS = 4
B_GROUP = B_HEADS // B_KV_HEADS
B_HEAD_DIM = 64
B_SCALE = B_HEAD_DIM ** -0.5
WINDOW = 128
WINDOW_CHUNKS = WINDOW // CHUNK
N_EXPERTS = 32
TOP_K = 4
D_FF = D_MODEL
SWIGLU_LIMIT = 7.0
SWIGLU_ALPHA = 1.702
MOE_BLOCK = 128

kernel_name = 'yoco_mla_swa_sink_moe_stream_step'


def rms_norm(x, gain):
    x32 = x.astype(jnp.float32)
    y = x32 * lax.rsqrt(jnp.mean(x32 * x32, axis=-1, keepdims=True) + EPS)
    return (y * gain.astype(jnp.float32)).astype(x.dtype)


def ada_mod(c, w, b, n):
    mod = jax.nn.silu(c) @ w + b
    return [m[:, None, :] for m in jnp.split(mod, n, axis=-1)]


def ada_norm(x, gain, shift, scale):
    return rms_norm(x, gain) * (1 + scale) + shift


def rope(x, pos):
    half = QK_ROPE // 2
    inv = ROPE_THETA ** (-jnp.arange(half, dtype=jnp.float32) / half)
    ang = pos.astype(jnp.float32)[:, None] * inv[None, :]
    shp = (1, pos.shape[0]) + (1,) * (x.ndim - 3) + (half,)
    cos = jnp.cos(ang).reshape(shp).astype(x.dtype)
    sin = jnp.sin(ang).reshape(shp).astype(x.dtype)
    x1, x2 = x[..., :half], x[..., half:]
    return jnp.concatenate([x1 * cos - x2 * sin, x1 * sin + x2 * cos], axis=-1)


def alibi_slopes(n):
    return 2.0 ** (-8.0 * jnp.arange(1, n + 1, dtype=jnp.float32) / n)


def mla_latent(h, pos, w_dkv, kv_gain):
    z = h @ w_dkv
    return rms_norm(z[..., :KV_LORA], kv_gain), rope(z[..., KV_LORA:], pos)


def mla_queries(h, pos, w_dq, q_gain, w_uq):
    B, T, _ = h.shape
    q = (rms_norm(h @ w_dq, q_gain) @ w_uq).reshape(B, T, A_HEADS, QK_HEAD)
    return jnp.concatenate([q[..., :QK_NOPE], rope(q[..., QK_NOPE:], pos)], axis=-1)


def mla_expand(ckv, kpe, w_ukv):
    B, S, _ = ckv.shape
    kv = (ckv @ w_ukv).reshape(B, S, A_HEADS, QK_NOPE + V_HEAD)
    k_pe = jnp.broadcast_to(kpe[:, :, None, :], (B, S, A_HEADS, QK_ROPE))
    return jnp.concatenate([kv[..., :QK_NOPE], k_pe], axis=-1), kv[..., QK_NOPE:]


def mla_attend(q, k, v, q_pos, k_pos):
    s = jnp.einsum('bthd,bshd->bhts', q, k).astype(jnp.float32) * MLA_SCALE
    visible = (k_pos[None, :] // CHUNK) <= (q_pos[:, None] // CHUNK)
    p = jax.nn.softmax(jnp.where(visible, s, -jnp.inf), axis=-1).astype(v.dtype)
    return jnp.einsum('bhts,bshd->bthd', p, v)


def mla_prompt(q, k, v, pos):
    B, S = q.shape[:2]
    nb = S // ATTN_Q_BLOCK
    qb = q.reshape(B, nb, ATTN_Q_BLOCK, A_HEADS, QK_HEAD).swapaxes(0, 1)
    pb = pos.reshape(nb, ATTN_Q_BLOCK)
    ob = lax.map(lambda a: mla_attend(a[0], k, v, a[1], pos), (qb, pb))
    return ob.swapaxes(0, 1).reshape(B, S, A_HEADS * V_HEAD)


def swa_attend(q, k, v, q_pos, k_pos, sinks):
    B, T = q.shape[:2]
    qg = q.reshape(B, T, B_KV_HEADS, B_GROUP, B_HEAD_DIM)
    s = jnp.einsum('btngd,bsnd->bngts', qg, k).astype(jnp.float32) * B_SCALE
    qc, kc = q_pos[:, None] // CHUNK, k_pos[None, :] // CHUNK
    visible = (kc <= qc) & (kc >= qc - WINDOW_CHUNKS) & (k_pos[None, :] >= 0)
    dist = jnp.abs(q_pos[:, None] - k_pos[None, :]).astype(jnp.float32)
    slopes = alibi_slopes(B_HEADS).reshape(B_KV_HEADS, B_GROUP, 1, 1)
    s = jnp.where(visible, s - slopes * dist, -jnp.inf)
    sink = sinks.astype(jnp.float32).reshape(B_KV_HEADS, B_GROUP, 1, 1)
    m = jnp.maximum(jnp.max(s, axis=-1, keepdims=True), sink)
    e = jnp.exp(s - m)
    p = (e / (jnp.sum(e, axis=-1, keepdims=True) + jnp.exp(sink - m))).astype(v.dtype)
    o = jnp.einsum('bngts,bsnd->btngd', p, v)
    return o.reshape(B, T, B_HEADS * B_HEAD_DIM)


def swa_bands(k, v, pos):
    S = k.shape[1]
    nc = S // CHUNK
    pad = WINDOW_CHUNKS * CHUNK
    idx = jnp.arange(nc)[:, None] * CHUNK + jnp.arange(pad + CHUNK)[None, :]
    widths = ((0, 0), (pad, 0), (0, 0), (0, 0))
    kb = jnp.pad(k, widths)[:, idx]
    vb = jnp.pad(v, widths)[:, idx]
    return kb, vb, idx - pad, pos.reshape(nc, CHUNK)


def moe_ffn(h, w_router, b_router, w1, b1, w2, b2):
    Bh, T, D = h.shape
    N = Bh * T
    x = h.reshape(N, D)
    logits = (x @ w_router + b_router).astype(jnp.float32)
    top_val, top_idx = lax.top_k(logits, TOP_K)
    gates = jax.nn.softmax(top_val, axis=-1)
    flat_e = top_idx.reshape(-1)
    order = jnp.argsort(flat_e, stable=True)
    sorted_e = flat_e[order]
    counts = jnp.zeros((N_EXPERTS,), jnp.int32).at[flat_e].add(1)
    padded = (counts + MOE_BLOCK - 1) // MOE_BLOCK * MOE_BLOCK
    start = jnp.cumsum(counts) - counts
    pend = jnp.cumsum(padded)
    pstart = pend - padded
    dest = pstart[sorted_e] + (jnp.arange(N * TOP_K) - start[sorted_e])
    n_blocks = -(-(N * TOP_K + N_EXPERTS * (MOE_BLOCK - 1)) // MOE_BLOCK)
    n_rows = n_blocks * MOE_BLOCK
    row_token = jnp.full((n_rows,), N, jnp.int32).at[dest].set(order // TOP_K)
    row_gate = jnp.zeros((n_rows,), jnp.float32).at[dest].set(gates.reshape(-1)[order])
    block_e = jnp.minimum(jnp.searchsorted(pend, jnp.arange(n_blocks) * MOE_BLOCK, side='right'), N_EXPERTS - 1)
    x_pad = jnp.concatenate([x, jnp.zeros((1, D), x.dtype)], axis=0)
    xb = x_pad[row_token].reshape(n_blocks, MOE_BLOCK, D)

    def expert_block(a):
        xi, e = a
        u = xi @ w1[e] + b1[e]
        glu = jnp.minimum(u[:, :D_FF], SWIGLU_LIMIT)
        lin = jnp.clip(u[:, D_FF:], -SWIGLU_LIMIT, SWIGLU_LIMIT)
        act = glu * jax.nn.sigmoid(SWIGLU_ALPHA * glu) * (lin + 1)
        return act @ w2[e] + b2[e]

    yb = lax.map(expert_block, (xb, block_e)).reshape(n_rows, D)
    y = jax.ops.segment_sum(yb * row_gate[:, None].astype(yb.dtype), row_token, num_segments=N + 1)[:N]
    return y.reshape(Bh, T, D)


def run_trunk(x, c, past, win_rows, P):
    B, T, _ = x.shape
    start = 0 if past is None else past[0].shape[2]
    pos = start + jnp.arange(T)
    new_ckv, new_kpe = [], []
    for l in range(DEPTH):
        if l == N_A_LAYERS:
            kv_shift, kv_scale = ada_mod(c, P['kv_ada_w'], P['kv_ada_b'], 2)
            hk = ada_norm(x, P['kv_norm'], kv_shift, kv_scale)
            k_new = (hk @ P['kv_w_k'] + P['kv_b_k']).reshape(B, T, B_KV_HEADS, B_HEAD_DIM)
            v_new = (hk @ P['kv_w_v'] + P['kv_b_v']).reshape(B, T, B_KV_HEADS, B_HEAD_DIM)
            if past is None:
                kb, vb, kpos_c, qpos_c = swa_bands(k_new, v_new, pos)
                k_state, v_state = k_new[:, T - win_rows:], v_new[:, T - win_rows:]
            else:
                L = past[2].shape[1]
                k_all = jnp.concatenate([past[2], k_new], axis=1)
                v_all = jnp.concatenate([past[3], v_new], axis=1)
                k_pos_all = start - L + jnp.arange(L + T)
                k_state, v_state = k_all[:, T:], v_all[:, T:]
        sh_m, sc_m, g_m, sh_f, sc_f, g_f = ada_mod(c, P['ada_w'][l], P['ada_b'][l], 6)
        h = ada_norm(x, P['norm_mix'][l], sh_m, sc_m)
        if l < N_A_LAYERS:
            q = mla_queries(h, pos, P['a_w_dq'][l], P['a_q_norm'][l], P['a_w_uq'][l])
            ckv, kpe = mla_latent(h, pos, P['a_w_dkv'][l], P['a_kv_norm'][l])
            new_ckv.append(ckv)
            new_kpe.append(kpe)
            if past is None:
                k, v = mla_expand(ckv, kpe, P['a_w_ukv'][l])
                o = mla_prompt(q, k, v, pos)
            else:
                ckv_all = jnp.concatenate([past[0][l], ckv], axis=1)
                kpe_all = jnp.concatenate([past[1][l], kpe], axis=1)
                k, v = mla_expand(ckv_all, kpe_all, P['a_w_ukv'][l])
                o = mla_attend(q, k, v, pos, jnp.arange(start + T)).reshape(B, T, A_HEADS * V_HEAD)
            mix = o @ P['a_w_o'][l]
        else:
            j = l - N_A_LAYERS
            q = (h @ P['b_w_q'][j] + P['b_b_q'][j]).reshape(B, T, B_HEADS, B_HEAD_DIM)
            sinks = P['b_sinks'][j]
            if past is None:
                qc = q.reshape(B, T // CHUNK, CHUNK, B_HEADS, B_HEAD_DIM)
                o = jax.vmap(lambda qi, ki, vi, qp, kp: swa_attend(qi, ki, vi, qp, kp, sinks),
                             in_axes=(1, 1, 1, 0, 0), out_axes=1)(qc, kb, vb, qpos_c, kpos_c)
                o = o.reshape(B, T, B_HEADS * B_HEAD_DIM)
            else:
                o = swa_attend(q, k_all, v_all, pos, k_pos_all, sinks)
            mix = o @ P['b_w_o'][j] + P['b_b_o'][j]
        x = x + g_m * mix
        h = ada_norm(x, P['norm_ffn'][l], sh_f, sc_f)
        x = x + g_f * moe_ffn(h, P['moe_w_router'][l], P['moe_b_router'][l], P['moe_w1'][l],
                              P['moe_b1'][l], P['moe_w2'][l], P['moe_b2'][l])
    y = rms_norm(x, P['final_norm'])
    return y, jnp.stack(new_ckv), jnp.stack(new_kpe), k_state, v_state


def setup_inputs(seed: int = 0) -> dict:
    key = jax.random.key(seed)
    ks = iter(jax.random.split(key, 64))
    d = D_MODEL
    win_rows = min(WINDOW, PAST_LEN)

    def nrm(shape, scale=1.0):
        return jax.random.normal(next(ks), shape, jnp.float32) * scale

    def gain(shape):
        return 1.0 + nrm(shape, 0.01)

    return {
        'x_prompt': nrm((BATCH, SEQ, d)),
        'x_sample': nrm((DEC_BATCH, DEC_SEQ, d)),
        'cache_a_ckv': nrm((N_A_LAYERS, DEC_BATCH, PAST_LEN, KV_LORA)),
        'cache_a_kpe': nrm((N_A_LAYERS, DEC_BATCH, PAST_LEN, QK_ROPE)),
        'cache_b_k': nrm((DEC_BATCH, win_rows, B_KV_HEADS, B_HEAD_DIM)),
        'cache_b_v': nrm((DEC_BATCH, win_rows, B_KV_HEADS, B_HEAD_DIM)),
        'c_prompt': nrm((BATCH, d)),
        'c_sample': nrm((DEC_BATCH, d)),
        'ada_w': nrm((DEPTH, d, 6 * d), 0.5 * d ** -0.5),
        'ada_b': nrm((DEPTH, 6 * d), 0.01),
        'norm_mix': gain((DEPTH, d)),
        'norm_ffn': gain((DEPTH, d)),
        'a_w_dq': nrm((N_A_LAYERS, d, Q_LORA), d ** -0.5),
        'a_q_norm': gain((N_A_LAYERS, Q_LORA)),
        'a_w_uq': nrm((N_A_LAYERS, Q_LORA, A_HEADS * QK_HEAD), Q_LORA ** -0.5),
        'a_w_dkv': nrm((N_A_LAYERS, d, KV_LORA + QK_ROPE), d ** -0.5),
        'a_kv_norm': gain((N_A_LAYERS, KV_LORA)),
        'a_w_ukv': nrm((N_A_LAYERS, KV_LORA, A_HEADS * (QK_NOPE + V_HEAD)), KV_LORA ** -0.5),
        'a_w_o': nrm((N_A_LAYERS, A_HEADS * V_HEAD, d), (A_HEADS * V_HEAD) ** -0.5),
        'kv_ada_w': nrm((d, 2 * d), 0.5 * d ** -0.5),
        'kv_ada_b': nrm((2 * d,), 0.01),
        'kv_norm': gain((d,)),
        'kv_w_k': nrm((d, B_KV_HEADS * B_HEAD_DIM), d ** -0.5),
        'kv_b_k': nrm((B_KV_HEADS * B_HEAD_DIM,), 0.01),
        'kv_w_v': nrm((d, B_KV_HEADS * B_HEAD_DIM), d ** -0.5),
        'kv_b_v': nrm((B_KV_HEADS * B_HEAD_DIM,), 0.01),
        'b_w_q': nrm((N_B_LAYERS, d, B_HEADS * B_HEAD_DIM), d ** -0.5),
        'b_b_q': nrm((N_B_LAYERS, B_HEADS * B_HEAD_DIM), 0.01),
        'b_sinks': nrm((N_B_LAYERS, B_HEADS), 1.0),
        'b_w_o': nrm((N_B_LAYERS, B_HEADS * B_HEAD_DIM, d), (B_HEADS * B_HEAD_DIM) ** -0.5),
        'b_b_o': nrm((N_B_LAYERS, d), 0.01),
        'moe_w_router': nrm((DEPTH, d, N_EXPERTS), d ** -0.5),
        'moe_b_router': nrm((DEPTH, N_EXPERTS), 0.01),
        'moe_w1': nrm((DEPTH, N_EXPERTS, d, 2 * D_FF), d ** -0.5),
        'moe_b1': nrm((DEPTH, N_EXPERTS, 2 * D_FF), 0.01),
        'moe_w2': nrm((DEPTH, N_EXPERTS, D_FF, d), D_FF ** -0.5),
        'moe_b2': nrm((DEPTH, N_EXPERTS, d), 0.01),
        'final_norm': gain((d,)),
    }


def reference(x_prompt, x_sample, cache_a_ckv, cache_a_kpe, cache_b_k, cache_b_v, c_prompt, c_sample,
              ada_w, ada_b, norm_mix, norm_ffn,
              a_w_dq, a_q_norm, a_w_uq, a_w_dkv, a_kv_norm, a_w_ukv, a_w_o,
              kv_ada_w, kv_ada_b, kv_norm, kv_w_k, kv_b_k, kv_w_v, kv_b_v,
              b_w_q, b_b_q, b_sinks, b_w_o, b_b_o,
              moe_w_router, moe_b_router, moe_w1, moe_b1, moe_w2, moe_b2, final_norm):
    P = dict(ada_w=ada_w, ada_b=ada_b, norm_mix=norm_mix, norm_ffn=norm_ffn,
             a_w_dq=a_w_dq, a_q_norm=a_q_norm, a_w_uq=a_w_uq, a_w_dkv=a_w_dkv,
             a_kv_norm=a_kv_norm, a_w_ukv=a_w_ukv, a_w_o=a_w_o,
             kv_ada_w=kv_ada_w, kv_ada_b=kv_ada_b, kv_norm=kv_norm,
             kv_w_k=kv_w_k, kv_b_k=kv_b_k, kv_w_v=kv_w_v, kv_b_v=kv_b_v,
             b_w_q=b_w_q, b_b_q=b_b_q, b_sinks=b_sinks, b_w_o=b_w_o, b_b_o=b_b_o,
             moe_w_router=moe_w_router, moe_b_router=moe_b_router, moe_w1=moe_w1,
             moe_b1=moe_b1, moe_w2=moe_w2, moe_b2=moe_b2, final_norm=final_norm)
    win_rows = cache_b_k.shape[1]
    y_prompt, p_ckv, p_kpe, p_k, p_v = run_trunk(x_prompt, c_prompt, None, win_rows, P)
    y_sample, s_ckv, s_kpe, s_k, s_v = run_trunk(
        x_sample, c_sample, (cache_a_ckv, cache_a_kpe, cache_b_k, cache_b_v), win_rows, P)
    return (y_prompt, y_sample, p_ckv, p_kpe, p_k, p_v, s_ckv, s_kpe, s_k, s_v)
```

```python
import functools

import jax
import jax.numpy as jnp
from jax import lax
from jax.experimental import pallas as pl
from jax.experimental.pallas import tpu as pltpu

F32 = jnp.float32
BF16 = jnp.bfloat16

CHUNK = 64
EPS = 1e-6
A_HEADS = 8
Q_LORA = 384
KV_LORA = 256
QK_NOPE = 128
QK_ROPE = 64
V_HEAD = 128
ROPE_THETA = 10000.0
MLA_SCALE = (QK_NOPE + QK_ROPE) ** -0.5
B_HEADS = 16
B_KV_HEADS = 4
B_GROUP = B_HEADS // B_KV_HEADS
B_HEAD_DIM = 64
B_SCALE = B_HEAD_DIM ** -0.5
WINDOW = 128
WINDOW_CHUNKS = WINDOW // CHUNK
TOP_K = 4
SWIGLU_LIMIT = 7.0
SWIGLU_ALPHA = 1.702

LANES = 128
NEG_BIG = -1e30
MOE_ROWS = 256
VMEM_LIMIT = 56 * 2 ** 20


def _cparams(*sem):
    return pltpu.CompilerParams(dimension_semantics=sem, vmem_limit_bytes=VMEM_LIMIT)


def _rms(x):
    return x * lax.rsqrt(jnp.mean(x * x, axis=-1, keepdims=True) + EPS)


def _dot(a, b):
    return jnp.dot(a, b, preferred_element_type=F32)


def _dot_nt(a, b):
    return lax.dot_general(a, b, (((1,), (1,)), ((), ())), preferred_element_type=F32)


def _ada_kernel(c_ref, w_ref, b_ref, o_ref):
    c = c_ref[...]
    s = c * jax.nn.sigmoid(c)
    o_ref[...] = _dot(s.astype(BF16), w_ref[...].astype(BF16)) + b_ref[...]


def ada_mod_all(c, w, b):
    L, D, M = w.shape
    Bc = c.shape[0]
    tn = 1024
    return pl.pallas_call(
        _ada_kernel,
        grid=(L, M // tn),
        in_specs=[
            pl.BlockSpec((Bc, D), lambda l, j: (0, 0)),
            pl.BlockSpec((None, D, tn), lambda l, j: (l, 0, j)),
            pl.BlockSpec((None, 1, tn), lambda l, j: (l, 0, j)),
        ],
        out_specs=pl.BlockSpec((None, Bc, tn), lambda l, j: (l, 0, j)),
        out_shape=jax.ShapeDtypeStruct((L, Bc, M), F32),
        compiler_params=_cparams("arbitrary", "arbitrary"),
        name="ada_mod",
    )(c, w, b.reshape(L, 1, M))


def _mod_spec(mod, tm, tiles_per_batch):
    R, D = mod.shape[1], mod.shape[2]
    return pl.BlockSpec((None, R, D), lambda i: (i // tiles_per_batch, 0, 0))


def _row_spec(tm, width):
    return pl.BlockSpec((tm, width), lambda i: (i, 0))


def _full_spec(shape):
    nd = len(shape)
    return pl.BlockSpec(shape, lambda i: (0,) * nd)


def _a_proj_kernel(x_ref, sh_ref, sc_ref, g_ref, wh_ref, qg_ref, wq_ref, kg_ref, cos_ref, sin_ref,
                   *rest, expand):
    if expand:
        wkv_ref, q_out, ckv_out, kpe_out, k_out, v_out = rest
    else:
        q_out, ckv_out, kpe_out = rest
    x = x_ref[...]
    h = _rms(x) * g_ref[...] * (1.0 + sc_ref[...]) + sh_ref[...]
    z = _dot(h.astype(BF16), wh_ref[...])
    qn = _rms(z[:, :Q_LORA]) * qg_ref[...]
    ckv = _rms(z[:, Q_LORA:Q_LORA + KV_LORA]) * kg_ref[...]
    cos = cos_ref[...]
    sin = sin_ref[...]
    o = Q_LORA + KV_LORA
    kpe = z[:, o:o + LANES] * cos + z[:, o + LANES:o + 2 * LANES] * sin
    ckv_out[...] = ckv
    kpe_out[...] = kpe[:, :QK_ROPE]
    qq = _dot(qn.astype(BF16), wq_ref[...])
    hw = A_HEADS * LANES
    for hd in range(A_HEADS):
        a = hd * LANES
        q_out[:, 2 * a:2 * a + LANES] = (qq[:, a:a + LANES] * MLA_SCALE).astype(BF16)
        pe = qq[:, hw + a:hw + a + LANES] * cos + qq[:, 2 * hw + a:2 * hw + a + LANES] * sin
        q_out[:, 2 * a + LANES:2 * a + 2 * LANES] = (pe * MLA_SCALE).astype(BF16)
    if expand:
        kv = _dot(ckv.astype(BF16), wkv_ref[...])
        kpe_b = kpe.astype(BF16)
        for hd in range(A_HEADS):
            a = hd * LANES
            k_out[:, 2 * a:2 * a + LANES] = kv[:, a:a + LANES].astype(BF16)
            k_out[:, 2 * a + LANES:2 * a + 2 * LANES] = kpe_b
        v_out[...] = kv[:, hw:].astype(BF16)


def a_proj(x, shift, scale, gain, wh, qgain, wq, kvgain, cos_t, sin_t, wkv, *, tm, tiles_per_batch, expand):
    N, D = x.shape
    nt = N // tm
    tab_tiles = cos_t.shape[0] // tm
    hw2 = 2 * A_HEADS * LANES
    in_specs = [
        _row_spec(tm, D),
        _mod_spec(shift, tm, tiles_per_batch),
        _mod_spec(scale, tm, tiles_per_batch),
        _full_spec(gain.shape),
        _full_spec(wh.shape),
        _full_spec(qgain.shape),
        _full_spec(wq.shape),
        _full_spec(kvgain.shape),
        pl.BlockSpec((tm, LANES), lambda i: (i % tab_tiles, 0)),
        pl.BlockSpec((tm, LANES), lambda i: (i % tab_tiles, 0)),
    ]
    args = [x, shift, scale, gain, wh, qgain, wq, kvgain, cos_t, sin_t]
    out_specs = [_row_spec(tm, hw2), _row_spec(tm, KV_LORA), _row_spec(tm, QK_ROPE)]
    out_shape = [jax.ShapeDtypeStruct((N, hw2), BF16), jax.ShapeDtypeStruct((N, KV_LORA), F32),
                 jax.ShapeDtypeStruct((N, QK_ROPE), F32)]
    if expand:
        in_specs.append(_full_spec(wkv.shape))
        args.append(wkv)
        out_specs += [_row_spec(tm, hw2), _row_spec(tm, A_HEADS * V_HEAD)]
        out_shape += [jax.ShapeDtypeStruct((N, hw2), BF16), jax.ShapeDtypeStruct((N, A_HEADS * V_HEAD), BF16)]
    return pl.pallas_call(
        functools.partial(_a_proj_kernel, expand=expand),
        grid=(nt,),
        in_specs=in_specs,
        out_specs=out_specs,
        out_shape=out_shape,
        compiler_params=_cparams("arbitrary"),
        name="a_proj",
    )(*args)


def _flash_kernel(q_ref, k_ref, v_ref, o_ref, m_sc, l_sc, acc_sc, *, tq):
    i = pl.program_id(2)
    q = q_ref[...]
    m_sc[...] = jnp.full(m_sc.shape, NEG_BIG, F32)
    l_sc[...] = jnp.zeros(l_sc.shape, F32)
    acc_sc[...] = jnp.zeros(acc_sc.shape, F32)

    def step(j, masked):
        off = pl.multiple_of(j * tq, tq)
        k = k_ref[pl.ds(off, tq), :]
        v = v_ref[pl.ds(off, tq), :]
        s = _dot_nt(q, k)
        if masked:
            qc = lax.broadcasted_iota(jnp.int32, (tq, tq), 0) // CHUNK
            kc = lax.broadcasted_iota(jnp.int32, (tq, tq), 1) // CHUNK
            s = jnp.where(kc <= qc, s, NEG_BIG)
        m_old = m_sc[...]
        m_new = jnp.maximum(m_old, jnp.max(s, axis=-1, keepdims=True))
        p = jnp.exp(s - m_new)
        alpha = jnp.exp(m_old - m_new)
        l_sc[...] = alpha * l_sc[...] + jnp.sum(p, axis=-1, keepdims=True)
        acc_sc[...] = alpha * acc_sc[...] + _dot(p.astype(BF16), v)
        m_sc[...] = m_new

    def body(j, c):
        step(j, False)
        return c

    lax.fori_loop(0, i, body, 0)
    step(i, True)
    o_ref[...] = (acc_sc[...] / l_sc[...]).astype(o_ref.dtype)


def mla_flash(q_cat, k_cat, v, *, tq):
    B, S, _ = q_cat.shape
    nq = S // tq
    return pl.pallas_call(
        functools.partial(_flash_kernel, tq=tq),
        grid=(B, A_HEADS, nq),
        in_specs=[
            pl.BlockSpec((None, tq, 2 * LANES), lambda b, h, i: (b, i, h)),
            pl.BlockSpec((None, S, 2 * LANES), lambda b, h, i: (b, 0, h)),
            pl.BlockSpec((None, S, LANES), lambda b, h, i: (b, 0, h)),
        ],
        out_specs=pl.BlockSpec((None, tq, LANES), lambda b, h, i: (b, i, h)),
        out_shape=jax.ShapeDtypeStruct((B, S, A_HEADS * V_HEAD), BF16),
        scratch_shapes=[pltpu.VMEM((tq, 1), F32), pltpu.VMEM((tq, 1), F32), pltpu.VMEM((tq, V_HEAD), F32)],
        compiler_params=_cparams("arbitrary", "arbitrary", "arbitrary"),
        name="mla_flash",
    )(q_cat, k_cat, v)


def _mla_sample_kernel(q_ref, cckv_ref, ckpe_ref, nckv_ref, nkpe_ref, wkv_ref, o_ref, *, past, t_new):
    cp = cckv_ref[...].astype(BF16)
    cn = nckv_ref[...].astype(BF16)
    kpe_p = ckpe_ref[...].astype(BF16)
    kpe_n = nkpe_ref[...].astype(BF16)
    qpos = past + lax.broadcasted_iota(jnp.int32, (t_new, t_new), 0)
    kpos = past + lax.broadcasted_iota(jnp.int32, (t_new, t_new), 1)
    vis_n = (kpos // CHUNK) <= (qpos // CHUNK)
    hw = A_HEADS * LANES
    for hd in range(A_HEADS):
        a = hd * LANES
        qn = q_ref[:, 2 * a:2 * a + LANES]
        qp = q_ref[:, 2 * a + LANES:2 * a + LANES + QK_ROPE]
        wk = wkv_ref[:, a:a + LANES]
        wv = wkv_ref[:, hw + a:hw + a + LANES]
        kn_p = _dot(cp, wk).astype(BF16)
        kn_n = _dot(cn, wk).astype(BF16)
        s_p = _dot_nt(qn, kn_p) + _dot_nt(qp, kpe_p)
        s_n = jnp.where(vis_n, _dot_nt(qn, kn_n) + _dot_nt(qp, kpe_n), NEG_BIG)
        m = jnp.maximum(jnp.max(s_p, axis=-1, keepdims=True), jnp.max(s_n, axis=-1, keepdims=True))
        e_p = jnp.exp(s_p - m)
        e_n = jnp.exp(s_n - m)
        l = jnp.sum(e_p, axis=-1, keepdims=True) + jnp.sum(e_n, axis=-1, keepdims=True)
        v_p = _dot(cp, wv).astype(BF16)
        v_n = _dot(cn, wv).astype(BF16)
        o = _dot(e_p.astype(BF16), v_p) + _dot(e_n.astype(BF16), v_n)
        o_ref[:, a:a + LANES] = (o / l).astype(o_ref.dtype)


def mla_sample(q_cat, cache_ckv, cache_kpe, new_ckv, new_kpe, wkv, *, t_new):
    Bd, past, _ = cache_ckv.shape
    assert past % CHUNK == 0
    return pl.pallas_call(
        functools.partial(_mla_sample_kernel, past=past, t_new=t_new),
        grid=(Bd,),
        in_specs=[
            pl.BlockSpec((t_new, 2 * A_HEADS * LANES), lambda b: (b, 0)),
            pl.BlockSpec((None, past, KV_LORA), lambda b: (b, 0, 0)),
            pl.BlockSpec((None, past, QK_ROPE), lambda b: (b, 0, 0)),
            pl.BlockSpec((t_new, KV_LORA), lambda b: (b, 0)),
            pl.BlockSpec((t_new, QK_ROPE), lambda b: (b, 0)),
            _full_spec(wkv.shape),
        ],
        out_specs=pl.BlockSpec((t_new, A_HEADS * V_HEAD), lambda b: (b, 0)),
        out_shape=jax.ShapeDtypeStruct((Bd * t_new, A_HEADS * V_HEAD), BF16),
        compiler_params=_cparams("arbitrary"),
        name="mla_sample",
    )(q_cat, cache_ckv, cache_kpe, new_ckv, new_kpe, wkv)


def _norm_proj_kernel(x_ref, sh_ref, sc_ref, g_ref, w_ref, b_ref, o_ref, *, out_scale):
    x = x_ref[...]
    h = _rms(x) * g_ref[...] * (1.0 + sc_ref[...]) + sh_ref[...]
    y = _dot(h.astype(BF16), w_ref[...]) + b_ref[...]
    if out_scale != 1.0:
        y = y * out_scale
    o_ref[...] = y.astype(o_ref.dtype)


def norm_proj(x, shift, scale, gain, w, b, *, tm, tiles_per_batch, out_scale, out_dtype):
    N, D = x.shape
    M = w.shape[1]
    return pl.pallas_call(
        functools.partial(_norm_proj_kernel, out_scale=out_scale),
        grid=(N // tm,),
        in_specs=[
            _row_spec(tm, D),
            _mod_spec(shift, tm, tiles_per_batch),
            _mod_spec(scale, tm, tiles_per_batch),
            _full_spec(gain.shape),
            _full_spec(w.shape),
            _full_spec(b.shape),
        ],
        out_specs=_row_spec(tm, M),
        out_shape=jax.ShapeDtypeStruct((N, M), out_dtype),
        compiler_params=_cparams("arbitrary"),
        name="norm_proj",
    )(x, shift, scale, gain, w, b)


def _swa_kernel(sink_ref, q_ref, kp_ref, kc_ref, vp_ref, vc_ref, o_ref, *, tq, pos0):
    start = pos0 + pl.program_id(1) * tq
    nk = WINDOW + tq
    q_pos = start + lax.broadcasted_iota(jnp.int32, (tq, nk), 0)
    k_pos = start - WINDOW + lax.broadcasted_iota(jnp.int32, (tq, nk), 1)
    qc = q_pos // CHUNK
    kc = (k_pos + WINDOW) // CHUNK - WINDOW_CHUNKS
    vis = (kc <= qc) & (kc >= qc - WINDOW_CHUNKS) & (k_pos >= 0)
    dist = jnp.abs(q_pos - k_pos).astype(F32)
    for n in range(B_KV_HEADS):
        c = n * B_HEAD_DIM
        k = jnp.concatenate([kp_ref[:, c:c + B_HEAD_DIM], kc_ref[:, c:c + B_HEAD_DIM]], axis=0).astype(BF16)
        v = jnp.concatenate([vp_ref[:, c:c + B_HEAD_DIM], vc_ref[:, c:c + B_HEAD_DIM]], axis=0).astype(BF16)
        for g in range(B_GROUP):
            hd = n * B_GROUP + g
            slope = 2.0 ** (-8.0 * (hd + 1) / B_HEADS)
            q = q_ref[:, hd * B_HEAD_DIM:(hd + 1) * B_HEAD_DIM]
            s = _dot_nt(q, k)
            s = jnp.where(vis, s - slope * dist, NEG_BIG)
            sink = sink_ref[hd]
            m = jnp.maximum(jnp.max(s, axis=-1, keepdims=True), sink)
            e = jnp.exp(s - m)
            denom = jnp.sum(e, axis=-1, keepdims=True) + jnp.exp(sink - m)
            o = _dot((e / denom).astype(BF16), v)
            o_ref[:, hd * B_HEAD_DIM:(hd + 1) * B_HEAD_DIM] = o.astype(o_ref.dtype)


def swa_attention(q, k_prev, k_cur, v_prev, v_cur, sinks, *, tq, pos0, prev_is_same):
    B, T, _ = q.shape
    kvw = B_KV_HEADS * B_HEAD_DIM
    nt = T // tq
    if prev_is_same:
        r = tq // WINDOW
        prev_map = lambda b, i, s: (b, jnp.maximum(i * r - 1, 0), 0)
    else:
        prev_map = lambda b, i, s: (b, 0, 0)
    cur_map = lambda b, i, s: (b, i, 0)
    return pl.pallas_call(
        functools.partial(_swa_kernel, tq=tq, pos0=pos0),
        grid_spec=pltpu.PrefetchScalarGridSpec(
            num_scalar_prefetch=1,
            grid=(B, nt),
            in_specs=[
                pl.BlockSpec((None, tq, B_HEADS * B_HEAD_DIM), cur_map),
                pl.BlockSpec((None, WINDOW, kvw), prev_map),
                pl.BlockSpec((None, tq, kvw), cur_map),
                pl.BlockSpec((None, WINDOW, kvw), prev_map),
                pl.BlockSpec((None, tq, kvw), cur_map),
            ],
            out_specs=pl.BlockSpec((None, tq, B_HEADS * B_HEAD_DIM), cur_map),
        ),
        out_shape=jax.ShapeDtypeStruct((B, T, B_HEADS * B_HEAD_DIM), BF16),
        compiler_params=_cparams("arbitrary", "arbitrary"),
        name="swa_attention",
    )(sinks, q, k_prev, k_cur, v_prev, v_cur)


def _post_attn_kernel(o_ref, wo_ref, bo_ref, x_ref, gm_ref, sh_ref, sc_ref, g_ref, wrh_ref, wrl_ref, br_ref,
                      x_out, h_out, gate_out, *, n_experts):
    mix = _dot(o_ref[...], wo_ref[...]) + bo_ref[...]
    x1 = x_ref[...] + gm_ref[...] * mix
    x_out[...] = x1
    h = _rms(x1) * g_ref[...] * (1.0 + sc_ref[...]) + sh_ref[...]
    h_out[...] = h
    h_hi = h.astype(BF16)
    h_lo = (h - h_hi.astype(F32)).astype(BF16)
    logits = _dot(h_hi, wrh_ref[...]) + _dot(h_lo, wrh_ref[...]) + _dot(h_hi, wrl_ref[...]) + br_ref[...]
    lane = lax.broadcasted_iota(jnp.int32, logits.shape, 1).astype(F32)
    vals = logits
    sel = jnp.zeros(logits.shape, jnp.bool_)
    top = None
    for _ in range(TOP_K):
        m = jnp.max(vals, axis=-1, keepdims=True)
        if top is None:
            top = m
        first = jnp.min(jnp.where(vals == m, lane, float(n_experts)), axis=-1, keepdims=True)
        pick = lane == first
        sel = sel | pick
        vals = jnp.where(pick, -jnp.inf, vals)
    e = jnp.where(sel, jnp.exp(logits - top), 0.0)
    gates = e / jnp.sum(e, axis=-1, keepdims=True)
    gate_out[...] = jnp.where(sel, gates, -1.0)


def post_attn(o, wo, bo, x, gm, shift, scale, gain, wr_hi, wr_lo, br, *, tm, tiles_per_batch):
    N, D = x.shape
    E = wr_hi.shape[1]
    return pl.pallas_call(
        functools.partial(_post_attn_kernel, n_experts=E),
        grid=(N // tm,),
        in_specs=[
            _row_spec(tm, o.shape[1]),
            _full_spec(wo.shape),
            _full_spec(bo.shape),
            _row_spec(tm, D),
            _mod_spec(gm, tm, tiles_per_batch),
            _mod_spec(shift, tm, tiles_per_batch),
            _mod_spec(scale, tm, tiles_per_batch),
            _full_spec(gain.shape),
            _full_spec(wr_hi.shape),
            _full_spec(wr_lo.shape),
            _full_spec(br.shape),
        ],
        out_specs=[_row_spec(tm, D), _row_spec(tm, D), _row_spec(tm, E)],
        out_shape=[jax.ShapeDtypeStruct((N, D), F32), jax.ShapeDtypeStruct((N, D), F32),
                   jax.ShapeDtypeStruct((N, E), F32)],
        compiler_params=_cparams("arbitrary"),
        name="post_attn",
    )(o, wo, bo, x, gm, shift, scale, gain, wr_hi, wr_lo, br)


def _moe_kernel(bstart_ref, bcount_ref, total_ref, idx_hbm, x_hbm, w1_ref, b1_ref, w2_ref, b2_ref, y_hbm,
                idx_sm, xbuf, ybuf, sem_i, sem_g, sem_s, *, tm, d_ff):
    e = pl.program_id(0)
    g0 = bstart_ref[e]
    total = total_ref[0]
    n_real = y_hbm.shape[0] - 2 * tm

    def idx_copy(g, slot):
        return pltpu.make_async_copy(idx_hbm.at[g], idx_sm.at[slot], sem_i.at[slot])

    def issue_gather(islot, bslot):
        def body(r8, c):
            for u in range(8):
                r = r8 * 8 + u
                t = idx_sm[islot, r]
                pltpu.make_async_copy(x_hbm.at[t], xbuf.at[bslot, r], sem_g.at[bslot]).start()
            return c
        lax.fori_loop(0, tm // 8, body, 0)

    def issue_scatter(islot, bslot):
        def body(r8, c):
            for u in range(8):
                r = r8 * 8 + u
                t = idx_sm[islot, tm + r]
                pltpu.make_async_copy(ybuf.at[bslot, r], y_hbm.at[t], sem_s.at[bslot]).start()
            return c
        lax.fori_loop(0, tm // 8, body, 0)

    def wait_gather(bslot):
        pltpu.make_async_copy(x_hbm.at[pl.ds(0, tm)], xbuf.at[bslot], sem_g.at[bslot]).wait()

    def wait_scatter(bslot):
        pltpu.make_async_copy(ybuf.at[bslot], y_hbm.at[pl.ds(0, tm)], sem_s.at[bslot]).wait()

    def block(j, c):
        g = g0 + j
        bslot = g % 2
        islot = g % 3

        @pl.when(g == 0)
        def _():
            ybuf[0] = jnp.zeros(ybuf.shape[1:], F32)
            for part in range(2):
                fill = pltpu.make_async_copy(ybuf.at[0], y_hbm.at[pl.ds(n_real + part * tm, tm)], sem_s.at[0])
                fill.start()
                fill.wait()
            idx_copy(0, 0).start()
            idx_copy(0, 0).wait()
            issue_gather(0, 0)

            @pl.when(total > 1)
            def _():
                idx_copy(1, 1).start()

        @pl.when(g + 1 < total)
        def _():
            nslot = (g + 1) % 3
            idx_copy(g + 1, nslot).wait()
            issue_gather(nslot, 1 - bslot)

            @pl.when(g + 2 < total)
            def _():
                idx_copy(g + 2, (g + 2) % 3).start()

        wait_gather(bslot)

        @pl.when(g >= 2)
        def _():
            wait_scatter(bslot)

        x = xbuf[bslot].astype(BF16)
        u = _dot(x, w1_ref[...]) + b1_ref[...]
        glu = jnp.minimum(u[:, :d_ff], SWIGLU_LIMIT)
        lin = jnp.clip(u[:, d_ff:], -SWIGLU_LIMIT, SWIGLU_LIMIT)
        act = glu * jax.nn.sigmoid(SWIGLU_ALPHA * glu) * (lin + 1.0)
        ybuf[bslot] = _dot(act.astype(BF16), w2_ref[...]) + b2_ref[...]
        issue_scatter(islot, bslot)

        @pl.when(g == total - 1)
        def _():
            wait_scatter(bslot)

            @pl.when(g >= 1)
            def _():
                wait_scatter(1 - bslot)

        return c

    lax.fori_loop(0, bcount_ref[e], block, 0)


def moe_experts(x_all, idx_tab, bstart, bcount, total, w1, b1, w2, b2, *, n_out_rows):
    N, D = x_all.shape
    E, _, F2 = w1.shape
    tm = MOE_ROWS
    return pl.pallas_call(
        functools.partial(_moe_kernel, tm=tm, d_ff=F2 // 2),
        grid_spec=pltpu.PrefetchScalarGridSpec(
            num_scalar_prefetch=3,
            grid=(E,),
            in_specs=[
                pl.BlockSpec(memory_space=pl.ANY),
                pl.BlockSpec(memory_space=pl.ANY),
                pl.BlockSpec((None, D, F2), lambda e, *_: (e, 0, 0)),
                pl.BlockSpec((None, 1, F2), lambda e, *_: (e, 0, 0)),
                pl.BlockSpec((None, F2 // 2, D), lambda e, *_: (e, 0, 0)),
                pl.BlockSpec((None, 1, D), lambda e, *_: (e, 0, 0)),
            ],
            out_specs=pl.BlockSpec(memory_space=pl.ANY),
            scratch_shapes=[
                pltpu.SMEM((3, 2 * tm), jnp.int32),
                pltpu.VMEM((2, tm, D), F32),
                pltpu.VMEM((2, tm, D), F32),
                pltpu.SemaphoreType.DMA((3,)),
                pltpu.SemaphoreType.DMA((2,)),
                pltpu.SemaphoreType.DMA((2,)),
            ],
        ),
        out_shape=jax.ShapeDtypeStruct((n_out_rows, D), F32),
        compiler_params=_cparams("arbitrary"),
        name="moe_experts",
    )(bstart, bcount, total, idx_tab, x_all, w1, b1, w2, b2)


def moe_plan(gate_map, n_experts):
    N = gate_map.shape[0]
    tm = MOE_ROWS
    sel = gate_map >= 0.0
    m = sel.astype(jnp.int32)
    counts = jnp.sum(m, axis=0)
    nblk = (counts + tm - 1) // tm
    bend = jnp.cumsum(nblk)
    bstart = bend - nblk
    rank = jnp.cumsum(m, axis=0) - m
    slot = jnp.cumsum(m, axis=1) - m
    dest = bstart[None, :] * tm + rank
    onehot = (slot[:, :, None] == jnp.arange(TOP_K)[None, None, :]) & sel[:, :, None]
    dest4 = jnp.sum(jnp.where(onehot, dest[:, :, None], 0), axis=1)
    gate4 = jnp.sum(jnp.where(onehot, gate_map[:, :, None], 0.0), axis=1)
    n_blocks = (N * TOP_K + n_experts * (tm - 1)) // tm + 1
    n_rows = n_blocks * tm
    rows = jnp.arange(n_rows, dtype=jnp.int32)
    flat = dest4.reshape(-1)
    src = jnp.zeros((n_rows,), jnp.int32).at[flat].set(jnp.arange(N * TOP_K, dtype=jnp.int32) // TOP_K,
                                                       unique_indices=True)
    dst = (N * TOP_K + rows % (2 * tm)).at[flat].set(jnp.arange(N * TOP_K, dtype=jnp.int32), unique_indices=True)
    idx_tab = jnp.concatenate([src.reshape(n_blocks, tm), dst.reshape(n_blocks, tm)], axis=1)
    total = bend[-1:].astype(jnp.int32)
    return idx_tab, bstart.astype(jnp.int32), nblk.astype(jnp.int32), total, gate4


def _combine_kernel(x_ref, y_ref, g4_ref, gf_ref, *rest, final):
    if final:
        fg_ref, o_ref = rest
    else:
        (o_ref,) = rest
    D = x_ref.shape[1]
    g4 = g4_ref[...]
    acc = g4[:, 0:1] * y_ref[:, 0:D]
    for s in range(1, TOP_K):
        acc = acc + g4[:, s:s + 1] * y_ref[:, s * D:(s + 1) * D]
    x2 = x_ref[...] + gf_ref[...] * acc
    if final:
        x2 = _rms(x2) * fg_ref[...]
    o_ref[...] = x2


def moe_combine(x, y4, gate4, gf, final_gain, *, tm, tiles_per_batch, row0):
    N, D = x.shape
    off = row0 // tm
    final = final_gain is not None
    in_specs = [
        _row_spec(tm, D),
        pl.BlockSpec((tm, TOP_K * D), lambda i: (i + off, 0)),
        _row_spec(tm, TOP_K),
        _mod_spec(gf, tm, tiles_per_batch),
    ]
    args = [x, y4, gate4, gf]
    if final:
        in_specs.append(_full_spec(final_gain.shape))
        args.append(final_gain)
    return pl.pallas_call(
        functools.partial(_combine_kernel, final=final),
        grid=(N // tm,),
        in_specs=in_specs,
        out_specs=_row_spec(tm, D),
        out_shape=jax.ShapeDtypeStruct((N, D), F32),
        compiler_params=_cparams("arbitrary"),
        name="moe_combine",
    )(*args)


def _rot_half(w):
    half = QK_ROPE // 2
    return jnp.concatenate([-w[..., half:], w[..., :half]], axis=-1)


def _pad_lanes(w):
    pad = [(0, 0)] * (w.ndim - 1) + [(0, LANES - w.shape[-1])]
    return jnp.pad(w, pad)


def _prep_a_weights(w_dq, w_uq, w_dkv, w_ukv):
    wc = w_dkv[:, :KV_LORA]
    wk = w_dkv[:, KV_LORA:]
    wh = jnp.concatenate([w_dq, wc, _pad_lanes(wk), _pad_lanes(_rot_half(wk))], axis=1).astype(BF16)
    uq = w_uq.reshape(Q_LORA, A_HEADS, QK_NOPE + QK_ROPE)
    nope = uq[:, :, :QK_NOPE].reshape(Q_LORA, A_HEADS * LANES)
    pe = uq[:, :, QK_NOPE:]
    wq = jnp.concatenate([nope, _pad_lanes(pe).reshape(Q_LORA, -1), _pad_lanes(_rot_half(pe)).reshape(Q_LORA, -1)],
                         axis=1).astype(BF16)
    ukv = w_ukv.reshape(KV_LORA, A_HEADS, QK_NOPE + V_HEAD)
    wkv = jnp.concatenate([ukv[:, :, :QK_NOPE].reshape(KV_LORA, -1), ukv[:, :, QK_NOPE:].reshape(KV_LORA, -1)],
                          axis=1).astype(BF16)
    return wh, wq, wkv


def _rope_tables(pos):
    half = QK_ROPE // 2
    inv = ROPE_THETA ** (-jnp.arange(half, dtype=F32) / half)
    ang = pos.astype(F32)[:, None] * inv[None, :]
    cos, sin = jnp.cos(ang), jnp.sin(ang)
    return _pad_lanes(jnp.concatenate([cos, cos], axis=1)), _pad_lanes(jnp.concatenate([sin, sin], axis=1))


def _split6(mods):
    return [m[:, None, :] for m in jnp.split(mods, 6, axis=-1)]


def _per_token(m, t):
    B, _, D = m.shape
    return jnp.broadcast_to(m, (B, t, D)).reshape(1, B * t, D)


def kernel(x_prompt, x_sample, cache_a_ckv, cache_a_kpe, cache_b_k, cache_b_v, c_prompt, c_sample, ada_w, ada_b, norm_mix, norm_ffn, a_w_dq, a_q_norm, a_w_uq, a_w_dkv, a_kv_norm, a_w_ukv, a_w_o, kv_ada_w, kv_ada_b, kv_norm, kv_w_k, kv_b_k, kv_w_v, kv_b_v, b_w_q, b_b_q, b_sinks, b_w_o, b_b_o, moe_w_router, moe_b_router, moe_w1, moe_b1, moe_w2, moe_b2, final_norm):
    B, S, D = x_prompt.shape
    Bd, Td, _ = x_sample.shape
    depth = ada_w.shape[0]
    n_a = a_w_dq.shape[0]
    E = moe_w_router.shape[2]
    past = cache_a_ckv.shape[2]
    win_rows = cache_b_k.shape[1]
    assert win_rows == WINDOW and S % CHUNK == 0 and Td <= CHUNK
    Np, Ns = B * S, Bd * Td
    kvw = B_KV_HEADS * B_HEAD_DIM

    tm_p = min(512, S)
    tpb_p = S // tm_p
    tq_a = min(512, S)
    tq_b = min(256, S)

    c_all = jnp.concatenate([c_prompt, c_sample], axis=0)
    mods = ada_mod_all(c_all, ada_w, ada_b)
    kv_mods = ada_mod_all(c_all, kv_ada_w[None], kv_ada_b[None])[0]

    pos_p = jnp.arange(S)
    pos_s = past + jnp.arange(Td)
    cos_p, sin_p = _rope_tables(pos_p)
    cos_s, sin_s = _rope_tables(pos_s)
    cos_s, sin_s = jnp.tile(cos_s, (Bd, 1)), jnp.tile(sin_s, (Bd, 1))

    xp = x_prompt.reshape(Np, D)
    xs = x_sample.reshape(Ns, D)
    row = lambda v: v.reshape(1, -1)

    w1_b = moe_w1.astype(BF16)
    w2_b = moe_w2.astype(BF16)

    p_ckv, p_kpe, s_ckv, s_kpe = [], [], [], []
    k_p = v_p = k_s = v_s = None
    y_p = y_s = None
    for l in range(depth):
        mp = _split6(mods[l, :B])
        ms = [_per_token(m, Td) for m in _split6(mods[l, B:])]
        gain_mix = row(norm_mix[l])
        if l < n_a:
            wh, wq, wkv = _prep_a_weights(a_w_dq[l], a_w_uq[l], a_w_dkv[l], a_w_ukv[l])
            qg, kg = row(a_q_norm[l]), row(a_kv_norm[l])
            q_c, ckv, kpe, k_c, v_c = a_proj(xp, mp[0], mp[1], gain_mix, wh, qg, wq, kg, cos_p, sin_p, wkv,
                                            tm=tm_p, tiles_per_batch=tpb_p, expand=True)
            p_ckv.append(ckv.reshape(B, S, KV_LORA))
            p_kpe.append(kpe.reshape(B, S, QK_ROPE))
            o_p = mla_flash(q_c.reshape(B, S, -1), k_c.reshape(B, S, -1), v_c.reshape(B, S, -1), tq=tq_a)
            o_p = o_p.reshape(Np, -1)
            q_s, ckv_n, kpe_n = a_proj(xs, ms[0], ms[1], gain_mix, wh, qg, wq, kg, cos_s, sin_s, None,
                                       tm=Ns, tiles_per_batch=1, expand=False)
            s_ckv.append(ckv_n.reshape(Bd, Td, KV_LORA))
            s_kpe.append(kpe_n.reshape(Bd, Td, QK_ROPE))
            o_s = mla_sample(q_s, cache_a_ckv[l], cache_a_kpe[l], ckv_n, kpe_n, wkv, t_new=Td)
            wo = a_w_o[l].astype(BF16)
            bo = jnp.zeros((1, D), F32)
        else:
            j = l - n_a
            if l == n_a:
                kvm = [m[:, None, :] for m in jnp.split(kv_mods, 2, axis=-1)]
                w_kv = jnp.concatenate([kv_w_k, kv_w_v], axis=1).astype(BF16)
                b_kv = row(jnp.concatenate([kv_b_k, kv_b_v]))
                g_kv = row(kv_norm)
                kv_p = norm_proj(xp, kvm[0][:B], kvm[1][:B], g_kv, w_kv, b_kv, tm=tm_p, tiles_per_batch=tpb_p,
                                 out_scale=1.0, out_dtype=F32)
                kv_s = norm_proj(xs, _per_token(kvm[0][B:], Td), _per_token(kvm[1][B:], Td), g_kv, w_kv, b_kv,
                                 tm=Ns, tiles_per_batch=1, out_scale=1.0, out_dtype=F32)
                k_p, v_p = kv_p[:, :kvw].reshape(B, S, kvw), kv_p[:, kvw:].reshape(B, S, kvw)
                k_s, v_s = kv_s[:, :kvw].reshape(Bd, Td, kvw), kv_s[:, kvw:].reshape(Bd, Td, kvw)
                ck = cache_b_k.reshape(Bd, win_rows, kvw)
                cv = cache_b_v.reshape(Bd, win_rows, kvw)
            wq_b = b_w_q[j].astype(BF16)
            bq_b = row(b_b_q[j])
            q_p = norm_proj(xp, mp[0], mp[1], gain_mix, wq_b, bq_b, tm=tm_p, tiles_per_batch=tpb_p,
                            out_scale=B_SCALE, out_dtype=BF16)
            q_s = norm_proj(xs, ms[0], ms[1], gain_mix, wq_b, bq_b, tm=Ns, tiles_per_batch=1,
                            out_scale=B_SCALE, out_dtype=BF16)
            o_p = swa_attention(q_p.reshape(B, S, -1), k_p, k_p, v_p, v_p, b_sinks[j], tq=tq_b, pos0=0,
                                prev_is_same=True).reshape(Np, -1)
            o_s = swa_attention(q_s.reshape(Bd, Td, -1), ck, k_s, cv, v_s, b_sinks[j], tq=Td, pos0=past,
                                prev_is_same=False).reshape(Ns, -1)
            wo = b_w_o[j].astype(BF16)
            bo = row(b_b_o[j])

        wr = moe_w_router[l]
        wr_hi = wr.astype(BF16)
        wr_lo = (wr - wr_hi.astype(F32)).astype(BF16)
        br = row(moe_b_router[l])
        gain_ffn = row(norm_ffn[l])
        x1_p, h_p, gm_p = post_attn(o_p, wo, bo, xp, mp[2], mp[3], mp[4], gain_ffn, wr_hi, wr_lo, br,
                                    tm=tm_p, tiles_per_batch=tpb_p)
        x1_s, h_s, gm_s = post_attn(o_s, wo, bo, xs, ms[2], ms[3], ms[4], gain_ffn, wr_hi, wr_lo, br,
                                    tm=Ns, tiles_per_batch=1)

        h_all = jnp.concatenate([h_p, h_s], axis=0)
        gmap = jnp.concatenate([gm_p, gm_s], axis=0)
        idx_tab, bstart, bcount, total, gate4 = moe_plan(gmap, E)
        n_all = Np + Ns
        y4 = moe_experts(h_all, idx_tab, bstart, bcount, total, w1_b[l], moe_b1[l][:, None, :], w2_b[l],
                         moe_b2[l][:, None, :], n_out_rows=n_all * TOP_K + 2 * MOE_ROWS)
        y4 = y4.reshape(-1, TOP_K * D)
        fg = row(final_norm) if l == depth - 1 else None
        xp = moe_combine(x1_p, y4, gate4[:Np], mp[5], fg, tm=tm_p, tiles_per_batch=tpb_p, row0=0)
        xs = moe_combine(x1_s, y4, gate4[Np:], ms[5], fg, tm=Ns, tiles_per_batch=1, row0=Np)

    y_p = xp.reshape(B, S, D)
    y_s = xs.reshape(Bd, Td, D)
    hd4 = (B_KV_HEADS, B_HEAD_DIM)
    pk = k_p[:, S - win_rows:].reshape(B, win_rows, *hd4)
    pv = v_p[:, S - win_rows:].reshape(B, win_rows, *hd4)
    sk = jnp.concatenate([ck, k_s], axis=1)[:, Td:].reshape(Bd, win_rows, *hd4)
    sv = jnp.concatenate([cv, v_s], axis=1)[:, Td:].reshape(Bd, win_rows, *hd4)
    return (y_p, y_s, jnp.stack(p_ckv), jnp.stack(p_kpe), pk, pv, jnp.stack(s_ckv), jnp.stack(s_kpe), sk, sv)
```
